```python
import jax, jax.numpy as jnp
from jax import lax
import numpy as np

D_MODEL = 1024
BATCH = 8
SEQ = 2048
DEPTH = 4
DEC_BATCH = 32
DEC_SEQ = 8
PAST_LEN = 16384
PAGE_SIZE = 128

H_A = 8
NOPE = 64
ROPE = 32
V_DIM = 64
Q_LORA = 256
KV_LORA = 128
ROPE_THETA = 10000.0
H_B = 8
D_B = 64
D_FF = ((8 * D_MODEL // 3 + 255) // 256) * 256
CONV_W = 3
PLE_DIM = 256
Q_BLOCK = 128
EPS = 1e-6
SPLIT_SIZES = (Q_LORA, KV_LORA, ROPE, H_B * D_B, H_B * D_B, H_B * D_B, D_MODEL, D_MODEL)
IN_COLS = Q_LORA + KV_LORA + ROPE + 3 * H_B * D_B + 2 * D_MODEL

kernel_name = 'mla_stickbreak_convffn_ple_hybrid_step'


def rmsnorm(x, g):
    xf = x.astype(jnp.float32)
    y = xf * lax.rsqrt(jnp.mean(xf * xf, axis=-1, keepdims=True) + EPS)
    return (y * g.astype(jnp.float32)).astype(x.dtype)


def split_cols(y):
    parts, start = [], 0
    for size in SPLIT_SIZES:
        parts.append(y[..., start:start + size])
        start += size
    return parts


def apply_rope(x, pos):
    half = x.shape[-1] // 2
    inv = ROPE_THETA ** (-jnp.arange(half, dtype=jnp.float32) / half)
    ang = pos.astype(jnp.float32)[:, None] * inv[None, :]
    shape = (1, pos.shape[0]) + (1,) * (x.ndim - 3) + (half,)
    cos, sin = jnp.cos(ang).reshape(shape), jnp.sin(ang).reshape(shape)
    x1 = x[..., :half].astype(jnp.float32)
    x2 = x[..., half:].astype(jnp.float32)
    return jnp.concatenate([x1 * cos - x2 * sin, x2 * cos + x1 * sin], axis=-1).astype(x.dtype)


def mla_attend(q_lat, q_rope, c_new, r_new, c_past, r_past):
    T = q_lat.shape[1]
    n_past = 0 if c_past is None else c_past.shape[1]
    scale = (NOPE + ROPE) ** -0.5
    outs = []
    for s in range(0, T, Q_BLOCK):
        e = min(s + Q_BLOCK, T)
        qa, qr = q_lat[:, s:e], q_rope[:, s:e]
        cn, rn = c_new[:, :e], r_new[:, :e]
        sc = (jnp.einsum('bqhr,bkr->bhqk', qa, cn, preferred_element_type=jnp.float32)
              + jnp.einsum('bqhd,bkd->bhqk', qr, rn, preferred_element_type=jnp.float32)) * scale
        causal = jnp.arange(e)[None, :] <= jnp.arange(s, e)[:, None]
        sc = jnp.where(causal, sc, -jnp.inf)
        if c_past is not None:
            sp = (jnp.einsum('bqhr,bkr->bhqk', qa, c_past, preferred_element_type=jnp.float32)
                  + jnp.einsum('bqhd,bkd->bhqk', qr, r_past, preferred_element_type=jnp.float32)) * scale
            sc = jnp.concatenate([sp, sc], axis=-1)
        w = jax.nn.softmax(sc, axis=-1).astype(c_new.dtype)
        o = jnp.einsum('bhqk,bkr->bqhr', w[..., n_past:], cn)
        if c_past is not None:
            o = o + jnp.einsum('bhqk,bkr->bqhr', w[..., :n_past], c_past)
        outs.append(o)
    return jnp.concatenate(outs, axis=1)


def sb_attend(q, k_new, v_new, k_past, v_past):
    T = q.shape[1]
    n_past = 0 if k_past is None else k_past.shape[1]
    scale = D_B ** -0.5
    outs = []
    for s in range(0, T, Q_BLOCK):
        e = min(s + Q_BLOCK, T)
        qb = q[:, s:e]
        kn, vn = k_new[:, :e], v_new[:, :e]
        z = jnp.einsum('bqhd,bkhd->bhqk', qb, kn, preferred_element_type=jnp.float32) * scale
        if k_past is not None:
            zp = jnp.einsum('bqhd,bkhd->bhqk', qb, k_past, preferred_element_type=jnp.float32) * scale
            z = jnp.concatenate([zp, z], axis=-1)
        kpos = jnp.arange(n_past + e)
        qpos = n_past + jnp.arange(s, e)
        mask = kpos[None, :] < qpos[:, None]
        log_1m = jnp.where(mask, jax.nn.log_sigmoid(-z), 0.0)
        tail = lax.cumsum(log_1m, axis=3, reverse=True) - log_1m
        a = jnp.where(mask, jnp.exp(jax.nn.log_sigmoid(z) + tail), 0.0).astype(v_new.dtype)
        o = jnp.einsum('bhqk,bkhd->bqhd', a[..., n_past:], vn)
        if k_past is not None:
            o = o + jnp.einsum('bhqk,bkhd->bqhd', a[..., :n_past], v_past)
        outs.append(o)
    return jnp.concatenate(outs, axis=1)


def causal_dwconv(u, prev, w, b):
    T = u.shape[1]
    ext = jnp.concatenate([prev, u], axis=1)
    y = b
    for k in range(CONV_W):
        y = y + ext[:, k:k + T] * w[k]
    return y, ext[:, -(CONV_W - 1):]


def gather_pages(cache, layer, page_table):
    g = cache[layer, page_table]
    return g.reshape((g.shape[0], g.shape[1] * g.shape[2]) + g.shape[3:])


def layer_step(h, p_l, pos, past, conv_prev, ln_mix, w_in, q_norm, w_uq, kv_norm, w_uk, w_uv,
               w_br_a, w_br_b, w_out, ln_ffn, w_up, conv_w, conv_b, w_down, ln_ple, w_ple_gate,
               w_ple_proj):
    B, T, _ = h.shape
    xn = rmsnorm(h, ln_mix)
    cq, ckv, kr, qb, kb, vb, ga, gb = split_cols(xn @ w_in)
    q = (rmsnorm(cq, q_norm) @ w_uq).reshape(B, T, H_A, NOPE + ROPE)
    q_nope, q_rope = q[..., :NOPE], apply_rope(q[..., NOPE:], pos)
    c_kv = rmsnorm(ckv, kv_norm)
    k_rope = apply_rope(kr, pos)
    q_lat = jnp.einsum('bthn,rhn->bthr', q_nope, w_uk)
    if past is None:
        c_past = r_past = kb_past = vb_past = None
    else:
        c_past, r_past, kb_past, vb_past = past
    o_lat = mla_attend(q_lat, q_rope, c_kv, k_rope, c_past, r_past)
    o_a = jnp.einsum('bthr,rhv->bthv', o_lat, w_uv).reshape(B, T, H_A * V_DIM)
    qb = qb.reshape(B, T, H_B, D_B)
    kb = kb.reshape(B, T, H_B, D_B)
    vb = vb.reshape(B, T, H_B, D_B)
    o_b = sb_attend(qb, kb, vb, kb_past, vb_past).reshape(B, T, H_B * D_B)
    mix = jax.nn.sigmoid(ga) * (o_a @ w_br_a) + jax.nn.sigmoid(gb) * (o_b @ w_br_b)
    h = h + mix @ w_out
    u = rmsnorm(h, ln_ffn) @ w_up
    u_conv, conv_state = causal_dwconv(u, conv_prev, conv_w, conv_b)
    gate, val = u_conv[..., :D_FF], u_conv[..., D_FF:]
    h = h + (jax.nn.silu(gate) * val) @ w_down
    h = h + jax.nn.sigmoid(rmsnorm(h, ln_ple) @ w_ple_gate) * (p_l @ w_ple_proj)
    return h, (c_kv, k_rope, kb, vb, conv_state)


def setup_inputs(seed: int = 0) -> dict:
    key = jax.random.key(seed)
    ks = iter(jax.random.split(key, 40))
    f32 = jnp.float32

    def nrm(shape, scale=1.0):
        return jax.random.normal(next(ks), shape, f32) * scale

    def gain(shape):
        return 1.0 + 0.02 * jax.random.normal(next(ks), shape, f32)

    n_pages = PAST_LEN // PAGE_SIZE
    n_used = DEC_BATCH * n_pages
    n_pool = n_used + n_used // 4
    page_table = jax.random.permutation(next(ks), n_pool)[:n_used].reshape(DEC_BATCH, n_pages).astype(jnp.int32)
    return {
        'x_prompt': nrm((BATCH, SEQ, D_MODEL)),
        'x_sample': nrm((DEC_BATCH, DEC_SEQ, D_MODEL)),
        'cache_mla_latent': nrm((DEPTH, n_pool, PAGE_SIZE, KV_LORA)),
        'cache_mla_rope': nrm((DEPTH, n_pool, PAGE_SIZE, ROPE)),
        'cache_sb_k': nrm((DEPTH, n_pool, PAGE_SIZE, H_B, D_B)),
        'cache_sb_v': nrm((DEPTH, n_pool, PAGE_SIZE, H_B, D_B)),
        'state_ffn_conv': nrm((DEPTH, DEC_BATCH, CONV_W - 1, 2 * D_FF)),
        'page_table': page_table,
        'p_prompt': nrm((DEPTH, BATCH, SEQ, PLE_DIM)),
        'p_sample': nrm((DEPTH, DEC_BATCH, DEC_SEQ, PLE_DIM)),
        'ln_mix': gain((DEPTH, D_MODEL)),
        'w_in': nrm((DEPTH, D_MODEL, IN_COLS), D_MODEL ** -0.5),
        'q_norm': gain((DEPTH, Q_LORA)),
        'w_uq': nrm((DEPTH, Q_LORA, H_A * (NOPE + ROPE)), Q_LORA ** -0.5),
        'kv_norm': gain((DEPTH, KV_LORA)),
        'w_uk': nrm((DEPTH, KV_LORA, H_A, NOPE), KV_LORA ** -0.5),
        'w_uv': nrm((DEPTH, KV_LORA, H_A, V_DIM), KV_LORA ** -0.5),
        'w_br_a': nrm((DEPTH, H_A * V_DIM, D_MODEL), (H_A * V_DIM) ** -0.5),
        'w_br_b': nrm((DEPTH, H_B * D_B, D_MODEL), (H_B * D_B) ** -0.5),
        'w_out': nrm((DEPTH, D_MODEL, D_MODEL), D_MODEL ** -0.5),
        'ln_ffn': gain((DEPTH, D_MODEL)),
        'w_up': nrm((DEPTH, D_MODEL, 2 * D_FF), D_MODEL ** -0.5),
        'conv_w': nrm((DEPTH, CONV_W, 2 * D_FF), CONV_W ** -0.5),
        'conv_b': nrm((DEPTH, 2 * D_FF), 0.01),
        'w_down': nrm((DEPTH, D_FF, D_MODEL), D_FF ** -0.5),
        'ln_ple': gain((DEPTH, D_MODEL)),
        'w_ple_gate': nrm((DEPTH, D_MODEL, D_MODEL), D_MODEL ** -0.5),
        'w_ple_proj': nrm((DEPTH, PLE_DIM, D_MODEL), PLE_DIM ** -0.5),
        'ln_final': gain((D_MODEL,)),
    }


def reference(x_prompt, x_sample, cache_mla_latent, cache_mla_rope, cache_sb_k, cache_sb_v,
              state_ffn_conv, page_table, p_prompt, p_sample, ln_mix, w_in, q_norm, w_uq, kv_norm,
              w_uk, w_uv, w_br_a, w_br_b, w_out, ln_ffn, w_up, conv_w, conv_b, w_down, ln_ple,
              w_ple_gate, w_ple_proj, ln_final):
    pos_p = jnp.arange(x_prompt.shape[1], dtype=jnp.int32)
    pos_s = PAST_LEN + jnp.arange(x_sample.shape[1], dtype=jnp.int32)
    conv_zero = jnp.zeros((x_prompt.shape[0], CONV_W - 1, 2 * D_FF), x_prompt.dtype)
    hp, hs = x_prompt, x_sample
    st_p, st_s = [], []
    for i in range(DEPTH):
        W = (ln_mix[i], w_in[i], q_norm[i], w_uq[i], kv_norm[i], w_uk[i], w_uv[i], w_br_a[i],
             w_br_b[i], w_out[i], ln_ffn[i], w_up[i], conv_w[i], conv_b[i], w_down[i], ln_ple[i],
             w_ple_gate[i], w_ple_proj[i])
        hp, sp = layer_step(hp, p_prompt[i], pos_p, None, conv_zero, *W)
        past = (gather_pages(cache_mla_latent, i, page_table),
                gather_pages(cache_mla_rope, i, page_table),
                gather_pages(cache_sb_k, i, page_table),
                gather_pages(cache_sb_v, i, page_table))
        hs, ss = layer_step(hs, p_sample[i], pos_s, past, state_ffn_conv[i], *W)
        st_p.append(sp)
        st_s.append(ss)
    y_prompt = rmsnorm(hp, ln_final)
    y_sample = rmsnorm(hs, ln_final)
    new_mla_latent_p = jnp.stack([s[0] for s in st_p])
    new_mla_rope_p = jnp.stack([s[1] for s in st_p])
    new_sb_k_p = jnp.stack([s[2] for s in st_p])
    new_sb_v_p = jnp.stack([s[3] for s in st_p])
    new_conv_p = jnp.stack([s[4] for s in st_p])
    new_mla_latent_s = jnp.stack([s[0] for s in st_s])
    new_mla_rope_s = jnp.stack([s[1] for s in st_s])
    new_sb_k_s = jnp.stack([s[2] for s in st_s])
    new_sb_v_s = jnp.stack([s[3] for s in st_s])
    new_conv_s = jnp.stack([s[4] for s in st_s])
    return (y_prompt, y_sample, new_mla_latent_p, new_mla_rope_p, new_sb_k_p, new_sb_v_p, new_conv_p,
            new_mla_latent_s, new_mla_rope_s, new_sb_k_s, new_sb_v_s, new_conv_s)
```

```python
import functools

import jax
import jax.numpy as jnp
from jax import lax
from jax.experimental import pallas as pl
from jax.experimental.pallas import tpu as pltpu

F32 = jnp.float32
BF16 = jnp.bfloat16

H_A, NOPE, ROPE, V_DIM = 8, 64, 32, 64
Q_LORA, KV_LORA = 256, 128
H_B, D_B = 8, 64
HD_B = H_B * D_B
CONV_W = 3
ROPE_THETA = 10000.0
EPS = 1e-6
MLA_SCALE = (NOPE + ROPE) ** -0.5
SB_SCALE = D_B ** -0.5

LANES = 128
QCAT = 2 * LANES
ROW_TILE = 256
ATT_TILE = 256
FF_CHUNK = 256
MLA_PAGES = 16
SB_PAGES = 8
SB_SUB = 256
VMEM_LIMIT = 56 * 1024 * 1024


def _cparams(*sem):
    return pltpu.CompilerParams(dimension_semantics=sem, vmem_limit_bytes=VMEM_LIMIT)


def _rms(x, g):
    return (x * lax.rsqrt(jnp.mean(x * x, axis=-1, keepdims=True) + EPS)) * g


def _dot(a, b):
    return jnp.dot(a, b, preferred_element_type=F32)


def _dot_t(a, b):
    return lax.dot_general(a, b, (((1,), (1,)), ((), ())), preferred_element_type=F32)


def _rope128(x, cos, sina, sinb):
    return x * cos + pltpu.roll(x, LANES - ROPE // 2, 1) * sina + pltpu.roll(x, ROPE // 2, 1) * sinb


def _neg_softplus(z):
    return -(jnp.maximum(z, 0.0) + jnp.log1p(jnp.exp(-jnp.abs(z))))


def _rev_cumsum(lg, tri):
    hi = lg.astype(BF16)
    lo = (lg - hi.astype(F32)).astype(BF16)
    return _dot(hi, tri) + _dot(lo, tri)


def _in_proj_kernel(h_ref, cos_ref, sina_ref, sinb_ref, ln_ref, win_ref, qn_ref, wuq_ref, kvn_ref,
                    wuk_ref, qcat_ref, kcat_ref, ckv_ref, krope_ref, qb_ref, kb_ref, vb_ref,
                    kb16_ref, vb16_ref):
    xn = _rms(h_ref[...], ln_ref[...]).astype(BF16)
    y = _dot(xn, win_ref[...])
    cos, sina, sinb = cos_ref[...], sina_ref[...], sinb_ref[...]

    cqn = _rms(y[:, :Q_LORA], qn_ref[...]).astype(BF16)
    q = _dot(cqn, wuq_ref[...])
    n_nope = H_A * NOPE
    q_lat = _dot(q[:, :n_nope].astype(BF16), wuk_ref[...])
    for h in range(H_A):
        xr = q[:, n_nope + h * LANES:n_nope + (h + 1) * LANES]
        qcat_ref[:, h * QCAT:h * QCAT + LANES] = q_lat[:, h * LANES:(h + 1) * LANES].astype(BF16)
        qcat_ref[:, h * QCAT + LANES:(h + 1) * QCAT] = _rope128(xr, cos, sina, sinb).astype(BF16)

    o = Q_LORA
    c_kv = _rms(y[:, o:o + KV_LORA], kvn_ref[...])
    ckv_ref[...] = c_kv
    o += KV_LORA
    kr = _rope128(y[:, o:o + LANES], cos, sina, sinb)
    krope_ref[...] = kr[:, :ROPE]
    kcat_ref[:, :LANES] = c_kv.astype(BF16)
    kcat_ref[:, LANES:] = kr.astype(BF16)
    o += LANES
    qb_ref[...] = (y[:, o:o + HD_B] * SB_SCALE).astype(BF16)
    o += HD_B
    kb = y[:, o:o + HD_B]
    kb_ref[...] = kb
    kb16_ref[...] = kb.astype(BF16)
    o += HD_B
    vb = y[:, o:o + HD_B]
    vb_ref[...] = vb
    vb16_ref[...] = vb.astype(BF16)


def _in_proj(h, tabs, ln, w_in_p, qn, w_uq_p, kvn, w_ukbd):
    n, d = h.shape
    tm = min(ROW_TILE, n)
    ntab = tabs[0].shape[0] // tm
    row = lambda i: (i, 0)
    const = lambda i: (0, 0)
    tab = lambda i: (i % ntab, 0)
    outs = [
        jax.ShapeDtypeStruct((n, H_A * QCAT), BF16),
        jax.ShapeDtypeStruct((n, QCAT), BF16),
        jax.ShapeDtypeStruct((n, KV_LORA), F32),
        jax.ShapeDtypeStruct((n, ROPE), F32),
        jax.ShapeDtypeStruct((n, HD_B), BF16),
        jax.ShapeDtypeStruct((n, HD_B), F32),
        jax.ShapeDtypeStruct((n, HD_B), F32),
        jax.ShapeDtypeStruct((n, HD_B), BF16),
        jax.ShapeDtypeStruct((n, HD_B), BF16),
    ]
    return pl.pallas_call(
        _in_proj_kernel,
        grid=(n // tm,),
        in_specs=[
            pl.BlockSpec((tm, d), row),
            pl.BlockSpec((tm, LANES), tab), pl.BlockSpec((tm, LANES), tab), pl.BlockSpec((tm, LANES), tab),
            pl.BlockSpec((1, d), const),
            pl.BlockSpec(w_in_p.shape, const),
            pl.BlockSpec((1, Q_LORA), const),
            pl.BlockSpec(w_uq_p.shape, const),
            pl.BlockSpec((1, KV_LORA), const),
            pl.BlockSpec(w_ukbd.shape, const),
        ],
        out_specs=[pl.BlockSpec((tm, s.shape[1]), row) for s in outs],
        out_shape=outs,
        compiler_params=_cparams("arbitrary"),
        name="in_proj",
    )(h, *tabs, ln, w_in_p, qn, w_uq_p, kvn, w_ukbd)


def _mla_prompt_kernel(q_ref, k_ref, o_ref, qs_ref, m_ref, l_ref, acc_ref, *, tq):
    qi = pl.program_id(1)
    rows = H_A * tq
    for h in range(H_A):
        qs_ref[h * tq:(h + 1) * tq, :] = q_ref[:, h * QCAT:(h + 1) * QCAT]
    m_ref[...] = jnp.full((rows, 1), -jnp.inf, F32)
    l_ref[...] = jnp.zeros((rows, 1), F32)
    acc_ref[...] = jnp.zeros((rows, KV_LORA), F32)

    def step(j, masked):
        kblk = k_ref[pl.ds(pl.multiple_of(j * tq, tq), tq), :]
        s = _dot_t(qs_ref[...], kblk) * MLA_SCALE
        if masked:
            r_t = lax.broadcasted_iota(jnp.int32, (rows, tq), 0) % tq
            col = lax.broadcasted_iota(jnp.int32, (rows, tq), 1)
            s = jnp.where(col <= r_t, s, -jnp.inf)
        m_prev = m_ref[...]
        m_new = jnp.maximum(m_prev, jnp.max(s, axis=1, keepdims=True))
        p = jnp.exp(s - m_new)
        alpha = jnp.exp(m_prev - m_new)
        l_ref[...] = alpha * l_ref[...] + jnp.sum(p, axis=1, keepdims=True)
        acc_ref[...] = alpha * acc_ref[...] + _dot(p.astype(BF16), kblk[:, :KV_LORA])
        m_ref[...] = m_new

    step(qi, True)

    def body(j, c):
        step(j, False)
        return c

    lax.fori_loop(0, qi, body, 0)
    o = acc_ref[...] / l_ref[...]
    for h in range(H_A):
        o_ref[:, h * KV_LORA:(h + 1) * KV_LORA] = o[h * tq:(h + 1) * tq].astype(BF16)


def _mla_prompt(qcat, kcat, nb, t):
    tq = min(ATT_TILE, t)
    nq = t // tq
    rows = H_A * tq
    return pl.pallas_call(
        functools.partial(_mla_prompt_kernel, tq=tq),
        grid=(nb, nq),
        in_specs=[
            pl.BlockSpec((tq, H_A * QCAT), lambda b, i: (b * nq + i, 0)),
            pl.BlockSpec((t, QCAT), lambda b, i: (b, 0)),
        ],
        out_specs=pl.BlockSpec((tq, H_A * KV_LORA), lambda b, i: (b * nq + i, 0)),
        out_shape=jax.ShapeDtypeStruct((nb * t, H_A * KV_LORA), BF16),
        scratch_shapes=[
            pltpu.VMEM((rows, QCAT), BF16),
            pltpu.VMEM((rows, 1), F32),
            pltpu.VMEM((rows, 1), F32),
            pltpu.VMEM((rows, KV_LORA), F32),
        ],
        compiler_params=_cparams("arbitrary", "arbitrary"),
        name="mla_prompt",
    )(qcat, kcat)


def _sb_prompt_kernel(q_ref, k_ref, v_ref, tri_ref, o_ref, acc_ref, carry_ref, *, tq):
    qi = pl.program_id(1)
    lane = lax.broadcasted_iota(jnp.int32, (1, LANES), 1)
    first = lane < D_B
    for pr in range(HD_B // LANES):
        cols = slice(pr * LANES, (pr + 1) * LANES)
        qp = q_ref[:, cols]
        zero = jnp.zeros_like(qp)
        qq = jnp.concatenate([jnp.where(first, qp, zero), jnp.where(first, zero, qp)], axis=0)
        acc_ref[...] = jnp.zeros((tq, LANES), F32)
        carry_ref[...] = jnp.zeros((2 * tq, 1), F32)

        def step(j, masked, cols=cols, qq=qq):
            rws = pl.ds(pl.multiple_of(j * tq, tq), tq)
            z = _dot_t(qq, k_ref[rws, cols])
            lg = _neg_softplus(z)
            if masked:
                r_t = lax.broadcasted_iota(jnp.int32, (2 * tq, tq), 0) % tq
                col = lax.broadcasted_iota(jnp.int32, (2 * tq, tq), 1)
                valid = col < r_t
                lg = jnp.where(valid, lg, 0.0)
            c = carry_ref[...]
            e = jnp.exp(z + _rev_cumsum(lg, tri_ref[...]) + c)
            if masked:
                e = jnp.where(valid, e, 0.0)
            r = _dot(e.astype(BF16), v_ref[rws, cols])
            acc_ref[...] += jnp.where(first, r[:tq], r[tq:])
            carry_ref[...] = c + jnp.sum(lg, axis=1, keepdims=True)

        step(qi, True)

        def body(i, c, step=step):
            step(qi - 1 - i, False)
            return c

        lax.fori_loop(0, qi, body, 0)
        o_ref[:, cols] = acc_ref[...].astype(BF16)


def _sb_prompt(qb, kb16, vb16, tri, nb, t):
    tq = min(ATT_TILE, t)
    nq = t // tq
    return pl.pallas_call(
        functools.partial(_sb_prompt_kernel, tq=tq),
        grid=(nb, nq),
        in_specs=[
            pl.BlockSpec((tq, HD_B), lambda b, i: (b * nq + i, 0)),
            pl.BlockSpec((t, HD_B), lambda b, i: (b, 0)),
            pl.BlockSpec((t, HD_B), lambda b, i: (b, 0)),
            pl.BlockSpec((tq, tq), lambda b, i: (0, 0)),
        ],
        out_specs=pl.BlockSpec((tq, HD_B), lambda b, i: (b * nq + i, 0)),
        out_shape=jax.ShapeDtypeStruct((nb * t, HD_B), BF16),
        scratch_shapes=[pltpu.VMEM((tq, LANES), F32), pltpu.VMEM((2 * tq, 1), F32)],
        compiler_params=_cparams("arbitrary", "arbitrary"),
        name="sb_prompt",
    )(qb, kb16, vb16, tri)


def _mla_sample_kernel(pt_ref, q_ref, kn_ref, lat_hbm, rope_hbm, o_ref, latbuf, ropebuf, kc_ref, sem,
                       *, layer, n_pages, t_new):
    b = pl.program_id(0)
    n_chunks = n_pages // MLA_PAGES
    page = latbuf.shape[1] // MLA_PAGES
    rows = H_A * t_new

    def copies(c, slot):
        out = []
        for i in range(MLA_PAGES):
            pg = pt_ref[b, c * MLA_PAGES + i]
            dst = pl.ds(i * page, page)
            out.append(pltpu.make_async_copy(lat_hbm.at[layer, pg], latbuf.at[slot, dst], sem.at[0, slot]))
            out.append(pltpu.make_async_copy(rope_hbm.at[layer, pg], ropebuf.at[slot, dst], sem.at[1, slot]))
        return out

    for cp in copies(0, 0):
        cp.start()

    qf = q_ref[0]
    qs = jnp.concatenate([qf[:, h * QCAT:(h + 1) * QCAT] for h in range(H_A)], axis=0).astype(BF16)
    kc_ref[:, LANES:] = jnp.zeros((kc_ref.shape[0], LANES), BF16)

    kn = kn_ref[0]
    s = _dot_t(qs, kn) * MLA_SCALE
    r_t = lax.broadcasted_iota(jnp.int32, s.shape, 0) % t_new
    col = lax.broadcasted_iota(jnp.int32, s.shape, 1)
    s = jnp.where(col <= r_t, s, -jnp.inf)
    m0 = jnp.max(s, axis=1, keepdims=True)
    p = jnp.exp(s - m0)
    l0 = jnp.sum(p, axis=1, keepdims=True)
    acc0 = _dot(p.astype(BF16), kn[:, :KV_LORA])

    def body(c, carry):
        m_prev, l_prev, acc = carry
        slot = c % 2

        @pl.when(c + 1 < n_chunks)
        def _():
            for cp in copies(c + 1, 1 - slot):
                cp.start()

        for cp in copies(c, slot):
            cp.wait()
        kc_ref[:, :LANES] = latbuf[slot].astype(BF16)
        kc_ref[:, LANES:LANES + ROPE] = ropebuf[slot].astype(BF16)
        kc = kc_ref[...]
        s = _dot_t(qs, kc) * MLA_SCALE
        m_new = jnp.maximum(m_prev, jnp.max(s, axis=1, keepdims=True))
        p = jnp.exp(s - m_new)
        alpha = jnp.exp(m_prev - m_new)
        l_new = alpha * l_prev + jnp.sum(p, axis=1, keepdims=True)
        acc = alpha * acc + _dot(p.astype(BF16), kc[:, :KV_LORA])
        return m_new, l_new, acc

    _, l_fin, acc = lax.fori_loop(0, n_chunks, body, (m0, l0, acc0))
    o = acc / l_fin
    for h in range(H_A):
        o_ref[0, :, h * KV_LORA:(h + 1) * KV_LORA] = o[h * t_new:(h + 1) * t_new]


def _mla_sample(page_table, qcat_f32, kn_pad, cache_lat, cache_rope, layer):
    nb, t_new, _ = qcat_f32.shape
    n_pages = page_table.shape[1]
    page = cache_lat.shape[2]
    ck = MLA_PAGES * page
    grid_spec = pltpu.PrefetchScalarGridSpec(
        num_scalar_prefetch=1,
        grid=(nb,),
        in_specs=[
            pl.BlockSpec((1, t_new, H_A * QCAT), lambda b, pt: (b, 0, 0)),
            pl.BlockSpec((1,) + kn_pad.shape[1:], lambda b, pt: (b, 0, 0)),
            pl.BlockSpec(memory_space=pl.ANY),
            pl.BlockSpec(memory_space=pl.ANY),
        ],
        out_specs=pl.BlockSpec((1, t_new, H_A * KV_LORA), lambda b, pt: (b, 0, 0)),
        scratch_shapes=[
            pltpu.VMEM((2, ck, KV_LORA), F32),
            pltpu.VMEM((2, ck, ROPE), F32),
            pltpu.VMEM((ck, QCAT), BF16),
            pltpu.SemaphoreType.DMA((2, 2)),
        ],
    )
    return pl.pallas_call(
        functools.partial(_mla_sample_kernel, layer=layer, n_pages=n_pages, t_new=t_new),
        grid_spec=grid_spec,
        out_shape=jax.ShapeDtypeStruct((nb, t_new, H_A * KV_LORA), F32),
        compiler_params=_cparams("arbitrary"),
        name="mla_sample",
    )(page_table, qcat_f32, kn_pad, cache_lat, cache_rope)


def _sb_sample_kernel(pt_ref, q_ref, kn_ref, vn_ref, tri_ref, k_hbm, v_hbm, o_ref, kbuf, vbuf, sem,
                      *, layer, n_pages, t_new):
    b = pl.program_id(0)
    n_chunks = n_pages // SB_PAGES
    page = kbuf.shape[1] // SB_PAGES
    n_sub = kbuf.shape[1] // SB_SUB
    rows = H_B * t_new

    def copies(c, slot):
        out = []
        for i in range(SB_PAGES):
            pg = pt_ref[b, c * SB_PAGES + i]
            dst = pl.ds(i * page, page)
            out.append(pltpu.make_async_copy(k_hbm.at[layer, pg], kbuf.at[slot, dst], sem.at[0, slot]))
            out.append(pltpu.make_async_copy(v_hbm.at[layer, pg], vbuf.at[slot, dst], sem.at[1, slot]))
        return out

    last = n_chunks - 1
    for cp in copies(last, last % 2):
        cp.start()

    qf = q_ref[0]
    qrep = jnp.concatenate([qf] * H_B, axis=0)
    r_h = lax.broadcasted_iota(jnp.int32, (rows, HD_B), 0) // t_new
    l_h = lax.broadcasted_iota(jnp.int32, (rows, HD_B), 1) // D_B
    own = r_h == l_h
    qbd = jnp.where(own, qrep, 0.0).astype(BF16)

    def block(kblk, vblk, tri, carry, acc, valid):
        z = _dot_t(qbd, kblk)
        lg = _neg_softplus(z)
        if valid is not None:
            lg = jnp.where(valid, lg, 0.0)
        e = jnp.exp(z + _rev_cumsum(lg, tri) + carry)
        if valid is not None:
            e = jnp.where(valid, e, 0.0)
        acc = acc + _dot(e.astype(BF16), vblk)
        return carry + jnp.sum(lg, axis=1, keepdims=True), acc

    n_new = kn_ref.shape[1]
    r_t = lax.broadcasted_iota(jnp.int32, (rows, n_new), 0) % t_new
    col = lax.broadcasted_iota(jnp.int32, (rows, n_new), 1)
    carry0, acc0 = block(kn_ref[0], vn_ref[0], tri_ref[:n_new, :n_new],
                         jnp.zeros((rows, 1), F32), jnp.zeros((rows, HD_B), F32), col < r_t)

    def body(i, st):
        carry, acc = st
        c = last - i
        slot = c % 2

        @pl.when(c > 0)
        def _():
            for cp in copies(c - 1, 1 - slot):
                cp.start()

        for cp in copies(c, slot):
            cp.wait()
        for sb in reversed(range(n_sub)):
            rws = slice(sb * SB_SUB, (sb + 1) * SB_SUB)
            carry, acc = block(kbuf[slot, rws, :].astype(BF16), vbuf[slot, rws, :].astype(BF16),
                               tri_ref[...], carry, acc, None)
        return carry, acc

    _, acc = lax.fori_loop(0, n_chunks, body, (carry0, acc0))
    acc = jnp.where(own, acc, 0.0)
    o = acc[:t_new]
    for h in range(1, H_B):
        o = o + acc[h * t_new:(h + 1) * t_new]
    o_ref[0] = o


def _sb_sample(page_table, qb_f32, kn_pad, vn_pad, tri, cache_k, cache_v, layer):
    nb, t_new, _ = qb_f32.shape
    n_pages = page_table.shape[1]
    page = cache_k.shape[2]
    ck = SB_PAGES * page
    grid_spec = pltpu.PrefetchScalarGridSpec(
        num_scalar_prefetch=1,
        grid=(nb,),
        in_specs=[
            pl.BlockSpec((1, t_new, HD_B), lambda b, pt: (b, 0, 0)),
            pl.BlockSpec((1,) + kn_pad.shape[1:], lambda b, pt: (b, 0, 0)),
            pl.BlockSpec((1,) + vn_pad.shape[1:], lambda b, pt: (b, 0, 0)),
            pl.BlockSpec(tri.shape, lambda b, pt: (0, 0)),
            pl.BlockSpec(memory_space=pl.ANY),
            pl.BlockSpec(memory_space=pl.ANY),
        ],
        out_specs=pl.BlockSpec((1, t_new, HD_B), lambda b, pt: (b, 0, 0)),
        scratch_shapes=[
            pltpu.VMEM((2, ck, HD_B), F32),
            pltpu.VMEM((2, ck, HD_B), F32),
            pltpu.SemaphoreType.DMA((2, 2)),
        ],
    )
    return pl.pallas_call(
        functools.partial(_sb_sample_kernel, layer=layer, n_pages=n_pages, t_new=t_new),
        grid_spec=grid_spec,
        out_shape=jax.ShapeDtypeStruct((nb, t_new, HD_B), F32),
        compiler_params=_cparams("arbitrary"),
        name="sb_sample",
    )(page_table, qb_f32, kn_pad, vn_pad, tri, cache_k, cache_v)


def _merge_kernel(h_ref, olat_ref, ob_ref, ln_ref, wg_ref, wuv_ref, wba_ref, wbb_ref, wout_ref, o_ref):
    x = h_ref[...]
    d = x.shape[1]
    xn = _rms(x, ln_ref[...]).astype(BF16)
    g = _dot(xn, wg_ref[...])
    o_a = _dot(olat_ref[...], wuv_ref[...]).astype(BF16)
    mix = (jax.nn.sigmoid(g[:, :d]) * _dot(o_a, wba_ref[...])
           + jax.nn.sigmoid(g[:, d:]) * _dot(ob_ref[...], wbb_ref[...]))
    o_ref[...] = x + _dot(mix.astype(BF16), wout_ref[...])


def _merge(h, o_lat, o_b, ln, w_g, w_uvbd, w_br_a, w_br_b, w_out):
    n, d = h.shape
    tm = min(ROW_TILE, n)
    row = lambda i: (i, 0)
    const = lambda i: (0, 0)
    return pl.pallas_call(
        _merge_kernel,
        grid=(n // tm,),
        in_specs=[
            pl.BlockSpec((tm, d), row),
            pl.BlockSpec((tm, o_lat.shape[1]), row),
            pl.BlockSpec((tm, o_b.shape[1]), row),
            pl.BlockSpec((1, d), const),
            pl.BlockSpec(w_g.shape, const),
            pl.BlockSpec(w_uvbd.shape, const),
            pl.BlockSpec(w_br_a.shape, const),
            pl.BlockSpec(w_br_b.shape, const),
            pl.BlockSpec(w_out.shape, const),
        ],
        out_specs=pl.BlockSpec((tm, d), row),
        out_shape=jax.ShapeDtypeStruct((n, d), F32),
        compiler_params=_cparams("arbitrary"),
        name="merge",
    )(h, o_lat, o_b, ln, w_g, w_uvbd, w_br_a, w_br_b, w_out)


def _ffn_kernel(*refs, d_ff, seq_tiles, seq_len, has_state, final):
    if has_state:
        (h_ref, p_ref, e1_ref, e2_ref, lnf_ref, wup_ref, cw_ref, cb_ref, wdn_ref, lnp_ref, wpg_ref,
         wpp_ref, lnz_ref, o_ref, tail_ref, act_ref, carry_ref) = refs
    else:
        (h_ref, p_ref, lnf_ref, wup_ref, cw_ref, cb_ref, wdn_ref, lnp_ref, wpg_ref,
         wpp_ref, lnz_ref, o_ref, tail_ref, act_ref, carry_ref) = refs
    i = pl.program_id(0)
    x = h_ref[...]
    tm = x.shape[0]
    xn = _rms(x, lnf_ref[...]).astype(BF16)
    row = lax.broadcasted_iota(jnp.int32, (tm, FF_CHUNK), 0) % seq_len
    seq_start = i % seq_tiles == 0

    @pl.when(i == 0)
    def _():
        carry_ref[...] = jnp.zeros(carry_ref.shape, F32)

    def conv(u, cols):
        if has_state:
            e1, e2 = e1_ref[:, cols], e2_ref[:, cols]
        else:
            c0 = jnp.where(seq_start, 0.0, carry_ref[6:7, cols])
            c1 = jnp.where(seq_start, 0.0, carry_ref[7:8, cols])
            e1 = jnp.broadcast_to(c1, u.shape)
            e2 = jnp.where(row == 0, c0, c1)
        u1 = jnp.where(row < 1, e1, pltpu.roll(u, 1, 0))
        u2 = jnp.where(row < 2, e2, pltpu.roll(u, 2, 0))
        y = cb_ref[:, cols] + u2 * cw_ref[0:1, cols]
        y = y + u1 * cw_ref[1:2, cols]
        y = y + u * cw_ref[2:3, cols]
        if has_state:
            tail_ref[:, cols] = u
        else:
            last8 = u[tm - 8:, :]
            tail_ref[0, :, cols] = last8
            carry_ref[:, cols] = last8
        return y

    for c in range(d_ff // FF_CHUNK):
        gc = slice(c * FF_CHUNK, (c + 1) * FF_CHUNK)
        vc = slice(d_ff + c * FF_CHUNK, d_ff + (c + 1) * FF_CHUNK)
        gate = conv(_dot(xn, wup_ref[:, gc]), gc)
        val = conv(_dot(xn, wup_ref[:, vc]), vc)
        act_ref[:, gc] = (gate * jax.nn.sigmoid(gate) * val).astype(BF16)

    y = x + _dot(act_ref[...], wdn_ref[...])
    yn = _rms(y, lnp_ref[...]).astype(BF16)
    y = y + jax.nn.sigmoid(_dot(yn, wpg_ref[...])) * _dot(p_ref[...].astype(BF16), wpp_ref[...])
    if final:
        y = _rms(y, lnz_ref[...])
    o_ref[...] = y


def _ffn(h, p, state_e, ln_ffn, w_up, conv_w, conv_b, w_down, ln_ple, w_pg, w_pp, ln_final, *,
         n_seq, final):
    n, d = h.shape
    c2 = w_up.shape[1]
    d_ff = c2 // 2
    tm = min(ROW_TILE, n)
    seq_len = n // n_seq
    has_state = state_e is not None
    row = lambda i: (i, 0)
    if has_state:
        seq_tiles = 1
        tail_spec = pl.BlockSpec((tm, c2), row)
        tail_shape = jax.ShapeDtypeStruct((n, c2), F32)
    else:
        seq_tiles = seq_len // tm
        tail_spec = pl.BlockSpec((1, 8, c2), lambda i: (i // seq_tiles, 0, 0))
        tail_shape = jax.ShapeDtypeStruct((n_seq, 8, c2), F32)
    const = lambda i: (0, 0)
    in_specs = [pl.BlockSpec((tm, d), row), pl.BlockSpec((tm, p.shape[1]), row)]
    args = [h, p]
    if has_state:
        in_specs += [pl.BlockSpec((tm, c2), row), pl.BlockSpec((tm, c2), row)]
        args += list(state_e)
    in_specs += [
        pl.BlockSpec((1, d), const),
        pl.BlockSpec(w_up.shape, const),
        pl.BlockSpec(conv_w.shape, const),
        pl.BlockSpec((1, c2), const),
        pl.BlockSpec(w_down.shape, const),
        pl.BlockSpec((1, d), const),
        pl.BlockSpec(w_pg.shape, const),
        pl.BlockSpec(w_pp.shape, const),
        pl.BlockSpec((1, d), const),
    ]
    args += [ln_ffn, w_up, conv_w, conv_b, w_down, ln_ple, w_pg, w_pp, ln_final]
    return pl.pallas_call(
        functools.partial(_ffn_kernel, d_ff=d_ff, seq_tiles=seq_tiles,
                          seq_len=seq_len if has_state else tm * seq_tiles,
                          has_state=has_state, final=final),
        grid=(n // tm,),
        in_specs=in_specs,
        out_specs=[pl.BlockSpec((tm, d), row), tail_spec],
        out_shape=[jax.ShapeDtypeStruct((n, d), F32), tail_shape],
        scratch_shapes=[pltpu.VMEM((tm, d_ff), BF16), pltpu.VMEM((8, c2), F32)],
        compiler_params=_cparams("arbitrary"),
        name="ffn",
    )(*args)


def _rope_tables(pos):
    half = ROPE // 2
    inv = ROPE_THETA ** (-jnp.arange(half, dtype=F32) / half)
    ang = pos.astype(F32)[:, None] * inv[None, :]
    cos, sin = jnp.cos(ang), jnp.sin(ang)
    z = jnp.zeros((pos.shape[0], LANES - ROPE), F32)
    zh = jnp.zeros_like(sin)
    return (jnp.concatenate([cos, cos, z], axis=1),
            jnp.concatenate([-sin, zh, z], axis=1),
            jnp.concatenate([zh, sin, z], axis=1))


def _prep_weights(w_in, w_uq, w_uk, w_uv):
    depth, d, _ = w_in.shape
    o_kr = Q_LORA + KV_LORA
    o_qb = o_kr + ROPE
    o_g = o_qb + 3 * HD_B
    w_in_p = jnp.concatenate(
        [w_in[:, :, :o_qb], jnp.zeros((depth, d, LANES - ROPE), w_in.dtype), w_in[:, :, o_qb:o_g]],
        axis=2).astype(BF16)
    w_g = w_in[:, :, o_g:].astype(BF16)
    uq = w_uq.reshape(depth, Q_LORA, H_A, NOPE + ROPE)
    uq_nope = uq[..., :NOPE].reshape(depth, Q_LORA, H_A * NOPE)
    uq_rope = jnp.pad(uq[..., NOPE:], ((0, 0), (0, 0), (0, 0), (0, LANES - ROPE)))
    w_uq_p = jnp.concatenate([uq_nope, uq_rope.reshape(depth, Q_LORA, H_A * LANES)], axis=2).astype(BF16)
    eye = jnp.eye(H_A, dtype=w_uk.dtype)
    w_ukbd = jnp.einsum('lrhn,hg->lhngr', w_uk, eye).reshape(depth, H_A * NOPE, H_A * KV_LORA).astype(BF16)
    w_uvbd = jnp.einsum('lrhv,hg->lhrgv', w_uv, eye).reshape(depth, H_A * KV_LORA, H_A * V_DIM).astype(BF16)
    return w_in_p, w_g, w_uq_p, w_ukbd, w_uvbd


def kernel(x_prompt, x_sample, cache_mla_latent, cache_mla_rope, cache_sb_k, cache_sb_v, state_ffn_conv,
           page_table, p_prompt, p_sample, ln_mix, w_in, q_norm, w_uq, kv_norm, w_uk, w_uv, w_br_a,
           w_br_b, w_out, ln_ffn, w_up, conv_w, conv_b, w_down, ln_ple, w_ple_gate, w_ple_proj,
           ln_final):
    nb, t, d = x_prompt.shape
    db, dt, _ = x_sample.shape
    depth = w_in.shape[0]
    n_pool, page = cache_sb_k.shape[1], cache_sb_k.shape[2]
    past_len = page_table.shape[1] * page
    c2 = w_up.shape[2]

    w_in_p, w_g, w_uq_p, w_ukbd, w_uvbd = _prep_weights(w_in, w_uq, w_uk, w_uv)
    w_br_a16, w_br_b16, w_out16 = w_br_a.astype(BF16), w_br_b.astype(BF16), w_out.astype(BF16)
    w_up16, w_down16 = w_up.astype(BF16), w_down.astype(BF16)
    w_pg16, w_pp16 = w_ple_gate.astype(BF16), w_ple_proj.astype(BF16)
    cache_k2 = cache_sb_k.reshape(depth, n_pool, page, HD_B)
    cache_v2 = cache_sb_v.reshape(depth, n_pool, page, HD_B)

    tabs_p = _rope_tables(jnp.arange(t, dtype=jnp.int32))
    tabs_s = tuple(jnp.tile(a, (db, 1)) for a in _rope_tables(past_len + jnp.arange(dt, dtype=jnp.int32)))
    tri_n = max(min(ATT_TILE, t), SB_SUB)
    tri = (jnp.arange(tri_n)[:, None] >= jnp.arange(tri_n)[None, :]).astype(BF16)
    lnz = ln_final.reshape(1, d)
    pad_new = lambda a: jnp.pad(a.reshape(db, dt, -1), ((0, 0), (0, LANES - dt), (0, 0)))

    hp = x_prompt.reshape(nb * t, d)
    hs = x_sample.reshape(db * dt, d)
    st_p, st_s = [], []
    for i in range(depth):
        vec = lambda a: a[i].reshape(1, -1)
        proj_w = (vec(ln_mix), w_in_p[i], vec(q_norm), w_uq_p[i], vec(kv_norm), w_ukbd[i])
        merge_w = (vec(ln_mix), w_g[i], w_uvbd[i], w_br_a16[i], w_br_b16[i], w_out16[i])
        ffn_w = (vec(ln_ffn), w_up16[i], conv_w[i], vec(conv_b), w_down16[i], vec(ln_ple), w_pg16[i],
                 w_pp16[i], lnz)
        final = i == depth - 1

        qcat, kcat, ckv, krope, qb, kb, vb, kb16, vb16 = _in_proj(hp, tabs_p, *proj_w)
        o_lat = _mla_prompt(qcat, kcat, nb, t)
        o_b = _sb_prompt(qb, kb16, vb16, tri[:min(ATT_TILE, t), :min(ATT_TILE, t)], nb, t)
        hp = _merge(hp, o_lat, o_b, *merge_w)
        hp, tail = _ffn(hp, p_prompt[i].reshape(nb * t, -1), None, *ffn_w, n_seq=nb, final=final)
        st_p.append((ckv.reshape(nb, t, KV_LORA), krope.reshape(nb, t, ROPE),
                     kb.reshape(nb, t, H_B, D_B), vb.reshape(nb, t, H_B, D_B), tail[:, 8 - (CONV_W - 1):]))

        qcat, kcat, ckv, krope, qb, kb, vb, kb16, vb16 = _in_proj(hs, tabs_s, *proj_w)
        o_lat = _mla_sample(page_table, qcat.astype(F32).reshape(db, dt, -1), pad_new(kcat),
                            cache_mla_latent, cache_mla_rope, i)
        o_b = _sb_sample(page_table, qb.astype(F32).reshape(db, dt, -1), pad_new(kb16), pad_new(vb16),
                         tri[:SB_SUB, :SB_SUB], cache_k2, cache_v2, i)
        hs = _merge(hs, o_lat.reshape(db * dt, -1).astype(BF16), o_b.reshape(db * dt, -1).astype(BF16),
                    *merge_w)
        st = state_ffn_conv[i]
        zeros = jnp.zeros((db, dt - 1, c2), F32)
        e1 = jnp.concatenate([st[:, 1:2], zeros], axis=1).reshape(db * dt, c2)
        e2 = jnp.concatenate([st, zeros[:, 1:]], axis=1).reshape(db * dt, c2)
        hs, tail = _ffn(hs, p_sample[i].reshape(db * dt, -1), (e1, e2), *ffn_w, n_seq=db, final=final)
        st_s.append((ckv.reshape(db, dt, KV_LORA), krope.reshape(db, dt, ROPE),
                     kb.reshape(db, dt, H_B, D_B), vb.reshape(db, dt, H_B, D_B),
                     tail.reshape(db, dt, c2)[:, dt - (CONV_W - 1):]))

    y_prompt = hp.reshape(nb, t, d)
    y_sample = hs.reshape(db, dt, d)
    outs_p = [jnp.stack([s[k] for s in st_p]) for k in range(5)]
    outs_s = [jnp.stack([s[k] for s in st_s]) for k in range(5)]
    return (y_prompt, y_sample, *outs_p, *outs_s)
```

```python
import functools

import jax
import jax.numpy as jnp
from jax import lax
from jax.experimental import pallas as pl
from jax.experimental.pallas import tpu as pltpu

F32 = jnp.float32
BF16 = jnp.bfloat16

H_A, NOPE, ROPE, V_DIM = 8, 64, 32, 64
Q_LORA, KV_LORA = 256, 128
H_B, D_B = 8, 64
HD_B = H_B * D_B
CONV_W = 3
ROPE_THETA = 10000.0
EPS = 1e-6
MLA_SCALE = (NOPE + ROPE) ** -0.5
SB_SCALE = D_B ** -0.5
SB_DEAD_LOG = -110.0

LANES = 128
SUBLANES = 8
QCAT = 2 * LANES
ROW_TILE = 256
ATT_TILE = 256
FF_CHUNK = 256
MLA_PAGES = 16
MLA_SLOTS = 3
SB_PAGES = 2
VMEM_LIMIT = 56 * 1024 * 1024


def _cparams(*sem):
    return pltpu.CompilerParams(dimension_semantics=sem, vmem_limit_bytes=VMEM_LIMIT)


def _rms(x, g):
    return (x * lax.rsqrt(jnp.mean(x * x, axis=-1, keepdims=True) + EPS)) * g


def _dot(a, b):
    return jnp.dot(a, b, preferred_element_type=F32)


def _dot_t(a, b):
    return lax.dot_general(a, b, (((1,), (1,)), ((), ())), preferred_element_type=F32)


def _rope128(x, cos, sina, sinb):
    return x * cos + pltpu.roll(x, LANES - ROPE // 2, 1) * sina + pltpu.roll(x, ROPE // 2, 1) * sinb


def _neg_softplus(z):
    return -(jnp.maximum(z, 0.0) + jnp.log(1.0 + jnp.exp(-jnp.abs(z))))


def _rev_cumsum(lg, tri):
    hi = lg.astype(BF16)
    lo = (lg - hi.astype(F32)).astype(BF16)
    return _dot(hi, tri) + _dot(lo, tri)


def _in_proj_kernel(*refs, stacked):
    (h_ref, cos_ref, sina_ref, sinb_ref, ln_ref, win_ref, qn_ref, wuq_ref, kvn_ref, wuk_ref) = refs[:10]
    outs = refs[13:] if stacked else refs[10:]
    qcat_ref, kcat_ref, krope_ref, qb_ref, kb16_ref, vb16_ref, ckvt_ref, ckv_ref, kb_ref, vb_ref = outs
    xn = _rms(h_ref[...], ln_ref[...]).astype(BF16)
    y = _dot(xn, win_ref[...])
    cos, sina, sinb = cos_ref[...], sina_ref[...], sinb_ref[...]

    cqn = _rms(y[:, :Q_LORA], qn_ref[...]).astype(BF16)
    q = _dot(cqn, wuq_ref[...])
    n_nope = H_A * NOPE
    q_lat = _dot(q[:, :n_nope].astype(BF16), wuk_ref[...])
    for h in range(H_A):
        xr = q[:, n_nope + h * LANES:n_nope + (h + 1) * LANES]
        qcat_ref[:, h * QCAT:h * QCAT + LANES] = q_lat[:, h * LANES:(h + 1) * LANES].astype(BF16)
        qcat_ref[:, h * QCAT + LANES:(h + 1) * QCAT] = _rope128(xr, cos, sina, sinb).astype(BF16)

    o = Q_LORA
    c_kv = _rms(y[:, o:o + KV_LORA], kvn_ref[...])
    if stacked:
        ckv_ref[0] = c_kv
    else:
        ckv_ref[...] = c_kv
    ckvt_ref[...] = c_kv.T.astype(BF16)
    o += KV_LORA
    kr = _rope128(y[:, o:o + LANES], cos, sina, sinb)
    krope_ref[...] = kr[:, :ROPE]
    kcat_ref[:, :LANES] = c_kv.astype(BF16)
    kcat_ref[:, LANES:] = kr.astype(BF16)
    o += LANES
    qb_ref[...] = (y[:, o:o + HD_B] * SB_SCALE).astype(BF16)
    o += HD_B
    kb = y[:, o:o + HD_B]
    kb16_ref[...] = kb.astype(BF16)
    o += HD_B
    vb = y[:, o:o + HD_B]
    vb16_ref[...] = vb.astype(BF16)
    if stacked:
        kb_ref[0, 0] = kb.T
        vb_ref[0, 0] = vb.T
    else:
        kb_ref[...] = kb
        vb_ref[...] = vb


def _in_proj(h, tabs, ln, w_in_p, qn, w_uq_p, kvn, w_ukbd, stacked=None, layer=None):
    n, d = h.shape
    tm = min(ROW_TILE, n)
    ntab = tabs[0].shape[0] // tm
    row = lambda i: (i, 0)
    const = lambda i: (0, 0)
    tab = lambda i: (i % ntab, 0)
    outs = [
        jax.ShapeDtypeStruct((n, H_A * QCAT), BF16),
        jax.ShapeDtypeStruct((n, QCAT), BF16),
        jax.ShapeDtypeStruct((n, ROPE), F32),
        jax.ShapeDtypeStruct((n, HD_B), BF16),
        jax.ShapeDtypeStruct((n, HD_B), BF16),
        jax.ShapeDtypeStruct((n, HD_B), BF16),
    ]
    out_specs = [pl.BlockSpec((tm, s.shape[1]), row) for s in outs]
    outs.append(jax.ShapeDtypeStruct((KV_LORA, n), BF16))
    out_specs.append(pl.BlockSpec((KV_LORA, tm), lambda i: (0, i)))
    in_specs = [
        pl.BlockSpec((tm, d), row),
        pl.BlockSpec((tm, LANES), tab), pl.BlockSpec((tm, LANES), tab), pl.BlockSpec((tm, LANES), tab),
        pl.BlockSpec((1, d), const),
        pl.BlockSpec(w_in_p.shape, const),
        pl.BlockSpec((1, Q_LORA), const),
        pl.BlockSpec(w_uq_p.shape, const),
        pl.BlockSpec((1, KV_LORA), const),
        pl.BlockSpec(w_ukbd.shape, const),
    ]
    args = [h, *tabs, ln, w_in_p, qn, w_uq_p, kvn, w_ukbd]
    aliases = {}
    if stacked is None:
        for width in (KV_LORA, HD_B, HD_B):
            outs.append(jax.ShapeDtypeStruct((n, width), F32))
            out_specs.append(pl.BlockSpec((tm, width), row))
    else:
        tps = stacked[1].shape[3] // tm
        aliases = {len(args) + k: len(outs) + k for k in range(3)}
        args += list(stacked)
        in_specs += [pl.BlockSpec(memory_space=pl.ANY)] * 3
        outs += [jax.ShapeDtypeStruct(a.shape, a.dtype) for a in stacked]
        out_specs.append(pl.BlockSpec((1, tm, KV_LORA), lambda i: (layer, i, 0)))
        kv_t = pl.BlockSpec((1, 1, HD_B, tm), lambda i: (layer, i // tps, 0, i % tps))
        out_specs += [kv_t, kv_t]
    return pl.pallas_call(
        functools.partial(_in_proj_kernel, stacked=stacked is not None),
        grid=(n // tm,),
        in_specs=in_specs,
        out_specs=out_specs,
        out_shape=outs,
        input_output_aliases=aliases,
        compiler_params=_cparams("arbitrary"),
        name="in_proj",
    )(*args)


def _mla_prompt_kernel(q_ref, k_ref, ct_ref, o_ref, qs_ref, m_ref, l_ref, acc_ref, *, tq):
    qi = pl.program_id(1)
    cols = H_A * tq
    for h in range(H_A):
        qs_ref[h * tq:(h + 1) * tq, :] = q_ref[:, h * QCAT:(h + 1) * QCAT]
    m_ref[...] = jnp.full((1, cols), -jnp.inf, F32)
    l_ref[...] = jnp.zeros((1, cols), F32)
    acc_ref[...] = jnp.zeros((KV_LORA, cols), F32)

    def step(j, masked):
        keys = pl.ds(pl.multiple_of(j * tq, tq), tq)
        s = _dot_t(k_ref[keys, :], qs_ref[...]) * MLA_SCALE
        if masked:
            key = lax.broadcasted_iota(jnp.int32, (tq, cols), 0)
            qry = lax.broadcasted_iota(jnp.int32, (tq, cols), 1) % tq
            s = jnp.where(key <= qry, s, -jnp.inf)
        m_prev = m_ref[...]
        m_new = jnp.maximum(m_prev, jnp.max(s, axis=0, keepdims=True))
        p = jnp.exp(s - m_new)
        alpha = jnp.exp(m_prev - m_new)
        l_ref[...] = alpha * l_ref[...] + jnp.sum(p, axis=0, keepdims=True)
        acc_ref[...] = alpha * acc_ref[...] + _dot(ct_ref[:, keys], p.astype(BF16))
        m_ref[...] = m_new

    step(qi, True)

    def body(j, c):
        step(j, False)
        return c

    lax.fori_loop(0, qi, body, 0)
    o = acc_ref[...] / l_ref[...]
    for h in range(H_A):
        o_ref[:, h * KV_LORA:(h + 1) * KV_LORA] = o[:, h * tq:(h + 1) * tq].T.astype(BF16)


def _mla_prompt(qcat, kcat, ckv_t, nb, t):
    tq = min(ATT_TILE, t)
    nq = t // tq
    cols = H_A * tq
    return pl.pallas_call(
        functools.partial(_mla_prompt_kernel, tq=tq),
        grid=(nb, nq),
        in_specs=[
            pl.BlockSpec((tq, H_A * QCAT), lambda b, i: (b * nq + i, 0)),
            pl.BlockSpec((t, QCAT), lambda b, i: (b, 0)),
            pl.BlockSpec((KV_LORA, t), lambda b, i: (0, b)),
        ],
        out_specs=pl.BlockSpec((tq, H_A * KV_LORA), lambda b, i: (b * nq + i, 0)),
        out_shape=jax.ShapeDtypeStruct((nb * t, H_A * KV_LORA), BF16),
        scratch_shapes=[
            pltpu.VMEM((cols, QCAT), BF16),
            pltpu.VMEM((1, cols), F32),
            pltpu.VMEM((1, cols), F32),
            pltpu.VMEM((KV_LORA, cols), F32),
        ],
        compiler_params=_cparams("arbitrary", "arbitrary"),
        name="mla_prompt",
    )(qcat, kcat, ckv_t)


def _sb_prompt_kernel(q_ref, k_ref, v_ref, tri_ref, o_ref, qq_ref, acc_ref, carry_ref, *, tq):
    qi = pl.program_id(1)
    lane = lax.broadcasted_iota(jnp.int32, (1, LANES), 1)
    first = lane < D_B
    n_pair = HD_B // LANES
    for pr in range(n_pair):
        qp = q_ref[:, pr * LANES:(pr + 1) * LANES]
        zero = jnp.zeros_like(qp)
        qq_ref[pr, :tq] = jnp.where(first, qp, zero)
        qq_ref[pr, tq:] = jnp.where(first, zero, qp)
    acc_ref[...] = jnp.zeros(acc_ref.shape, F32)
    carry_ref[...] = jnp.zeros(carry_ref.shape, F32)

    def step(j, masked):
        rws = pl.ds(pl.multiple_of(j * tq, tq), tq)
        for pr in range(n_pair):
            cols = slice(pr * LANES, (pr + 1) * LANES)
            z = _dot_t(qq_ref[pr], k_ref[rws, cols])
            lg = _neg_softplus(z)
            if masked:
                r_t = lax.broadcasted_iota(jnp.int32, (2 * tq, tq), 0) % tq
                col = lax.broadcasted_iota(jnp.int32, (2 * tq, tq), 1)
                valid = col < r_t
                lg = jnp.where(valid, lg, 0.0)
            c = carry_ref[pr]
            e = jnp.exp(z + _rev_cumsum(lg, tri_ref[...]) + c)
            if masked:
                e = jnp.where(valid, e, 0.0)
            r = _dot(e.astype(BF16), v_ref[rws, cols])
            acc_ref[pr] += jnp.where(first, r[:tq], r[tq:])
            carry_ref[pr] = c + jnp.sum(lg, axis=1, keepdims=True)

    step(qi, True)

    def live():
        return jnp.max(carry_ref[...]) > SB_DEAD_LOG

    def cond(st):
        return jnp.logical_and(st[0] < qi, st[1])

    def body(st):
        step(qi - 1 - st[0], False)
        return st[0] + 1, live()

    lax.while_loop(cond, body, (jnp.int32(0), live()))
    for pr in range(n_pair):
        o_ref[:, pr * LANES:(pr + 1) * LANES] = acc_ref[pr].astype(BF16)


def _sb_prompt(qb, kb16, vb16, tri, nb, t):
    tq = min(ATT_TILE, t)
    nq = t // tq
    return pl.pallas_call(
        functools.partial(_sb_prompt_kernel, tq=tq),
        grid=(nb, nq),
        in_specs=[
            pl.BlockSpec((tq, HD_B), lambda b, i: (b * nq + i, 0)),
            pl.BlockSpec((t, HD_B), lambda b, i: (b, 0)),
            pl.BlockSpec((t, HD_B), lambda b, i: (b, 0)),
            pl.BlockSpec((tq, tq), lambda b, i: (0, 0)),
        ],
        out_specs=pl.BlockSpec((tq, HD_B), lambda b, i: (b * nq + i, 0)),
        out_shape=jax.ShapeDtypeStruct((nb * t, HD_B), BF16),
        scratch_shapes=[
            pltpu.VMEM((HD_B // LANES, 2 * tq, LANES), BF16),
            pltpu.VMEM((HD_B // LANES, tq, LANES), F32),
            pltpu.VMEM((HD_B // LANES, 2 * tq, 1), F32),
        ],
        compiler_params=_cparams("arbitrary", "arbitrary"),
        name="sb_prompt",
    )(qb, kb16, vb16, tri)


def _mla_sample_kernel(pt_ref, q_ref, kn_ref, lat_hbm, rope_hbm, o_ref, latbuf, ropebuf, sem,
                       *, layer, n_pages, t_new):
    b = pl.program_id(0)
    n_chunks = n_pages // MLA_PAGES
    n_total = pl.num_programs(0) * n_chunks
    page = latbuf.shape[1] // MLA_PAGES

    def copies(g):
        bb, c, slot = g // n_chunks, g % n_chunks, g % MLA_SLOTS
        out = []
        for i in range(MLA_PAGES):
            pg = pt_ref[bb, c * MLA_PAGES + i]
            dst = pl.ds(i * page, page)
            out.append(pltpu.make_async_copy(lat_hbm.at[layer, pg], latbuf.at[slot, dst], sem.at[0, slot]))
            out.append(pltpu.make_async_copy(rope_hbm.at[layer, pg], ropebuf.at[slot, :, dst], sem.at[1, slot]))
        return out

    @pl.when(b == 0)
    def _():
        for g in range(MLA_SLOTS - 1):
            for cp in copies(g):
                cp.start()

    qf = q_ref[0]
    qs = jnp.concatenate([qf[:, h * QCAT:(h + 1) * QCAT] for h in range(H_A)], axis=0).astype(BF16)
    q_lat, q_rope = qs[:, :KV_LORA], qs[:, KV_LORA:KV_LORA + ROPE]

    kn = kn_ref[0]
    s = _dot_t(qs, kn) * MLA_SCALE
    r_t = lax.broadcasted_iota(jnp.int32, s.shape, 0) % t_new
    col = lax.broadcasted_iota(jnp.int32, s.shape, 1)
    s = jnp.where(col <= r_t, s, -jnp.inf)
    m0 = jnp.max(s, axis=1, keepdims=True)
    p = jnp.exp(s - m0)
    l0 = jnp.sum(p, axis=1, keepdims=True)
    acc0 = _dot(p.astype(BF16), kn[:, :KV_LORA])

    def body(c, carry):
        m_prev, l_prev, acc = carry
        g = b * n_chunks + c
        slot = g % MLA_SLOTS
        ahead = g + MLA_SLOTS - 1

        @pl.when(ahead < n_total)
        def _():
            for cp in copies(ahead):
                cp.start()

        for cp in copies(g):
            cp.wait()
        lat = latbuf[slot].astype(BF16)
        s = (_dot_t(q_lat, lat) + _dot(q_rope, ropebuf[slot].astype(BF16))) * MLA_SCALE
        m_new = jnp.maximum(m_prev, jnp.max(s, axis=1, keepdims=True))
        p = jnp.exp(s - m_new)
        alpha = jnp.exp(m_prev - m_new)
        l_new = alpha * l_prev + jnp.sum(p, axis=1, keepdims=True)
        acc = alpha * acc + _dot(p.astype(BF16), lat)
        return m_new, l_new, acc

    _, l_fin, acc = lax.fori_loop(0, n_chunks, body, (m0, l0, acc0))
    o = acc / l_fin
    for h in range(H_A):
        o_ref[0, :, h * KV_LORA:(h + 1) * KV_LORA] = o[h * t_new:(h + 1) * t_new]


def _mla_sample(page_table, qcat_f32, kn_pad, cache_lat, cache_rope, layer):
    nb, t_new, _ = qcat_f32.shape
    n_pages = page_table.shape[1]
    page = cache_lat.shape[2]
    ck = MLA_PAGES * page
    grid_spec = pltpu.PrefetchScalarGridSpec(
        num_scalar_prefetch=1,
        grid=(nb,),
        in_specs=[
            pl.BlockSpec((1, t_new, H_A * QCAT), lambda b, pt: (b, 0, 0)),
            pl.BlockSpec((1,) + kn_pad.shape[1:], lambda b, pt: (b, 0, 0)),
            pl.BlockSpec(memory_space=pl.ANY),
            pl.BlockSpec(memory_space=pl.ANY),
        ],
        out_specs=pl.BlockSpec((1, t_new, H_A * KV_LORA), lambda b, pt: (b, 0, 0)),
        scratch_shapes=[
            pltpu.VMEM((MLA_SLOTS, ck, KV_LORA), F32),
            pltpu.VMEM((MLA_SLOTS, ROPE, ck), F32),
            pltpu.SemaphoreType.DMA((2, MLA_SLOTS)),
        ],
    )
    return pl.pallas_call(
        functools.partial(_mla_sample_kernel, layer=layer, n_pages=n_pages, t_new=t_new),
        grid_spec=grid_spec,
        out_shape=jax.ShapeDtypeStruct((nb, t_new, H_A * KV_LORA), F32),
        compiler_params=_cparams("arbitrary"),
        name="mla_sample",
    )(page_table, qcat_f32, kn_pad, cache_lat, cache_rope)


def _sb_sample_kernel(pt_ref, q_ref, kn_ref, vn_ref, tri_ref, k_hbm, v_hbm, o_ref, kbuf, vbuf, sem,
                      *, layer, n_pages, t_new):
    b = pl.program_id(0)
    n_chunks = n_pages // SB_PAGES
    rows = H_B * t_new

    last = n_chunks - 1

    def slot_of(bb, i):
        return 2 * (bb % 2) + i % 2

    def copies(bb, i):
        slot = slot_of(bb, i)
        out = []
        for pg_i in range(SB_PAGES):
            pg = pt_ref[bb, (last - i) * SB_PAGES + pg_i]
            out.append(pltpu.make_async_copy(k_hbm.at[layer, pg], kbuf.at[slot, pg_i], sem.at[0, slot]))
            out.append(pltpu.make_async_copy(v_hbm.at[layer, pg], vbuf.at[slot, pg_i], sem.at[1, slot]))
        return out

    @pl.when(b == 0)
    def _():
        for cp in copies(0, 0):
            cp.start()

    @pl.when(b + 1 < pl.num_programs(0))
    def _():
        for cp in copies(b + 1, 0):
            cp.start()

    qf = q_ref[0]
    qrep = jnp.concatenate([qf] * H_B, axis=0)
    r_h = lax.broadcasted_iota(jnp.int32, (rows, HD_B), 0) // t_new
    l_h = lax.broadcasted_iota(jnp.int32, (rows, HD_B), 1) // D_B
    own = r_h == l_h
    qbd = jnp.where(own, qrep, 0.0).astype(BF16)

    def block(kblk, vblk, carry, acc, valid, keys_on_lanes):
        z = _dot(qbd, kblk) if keys_on_lanes else _dot_t(qbd, kblk)
        lg = _neg_softplus(z)
        if valid is not None:
            lg = jnp.where(valid, lg, 0.0)
        e = jnp.exp(z + _rev_cumsum(lg, tri_ref[...]) + carry)
        if valid is not None:
            e = jnp.where(valid, e, 0.0)
        e = e.astype(BF16)
        acc = acc + (_dot_t(e, vblk) if keys_on_lanes else _dot(e, vblk))
        return carry + jnp.sum(lg, axis=1, keepdims=True), acc

    n_new = kn_ref.shape[1]
    r_t = lax.broadcasted_iota(jnp.int32, (rows, n_new), 0) % t_new
    col = lax.broadcasted_iota(jnp.int32, (rows, n_new), 1)
    carry0, acc0 = block(kn_ref[0], vn_ref[0], jnp.zeros((rows, 1), F32), jnp.zeros((rows, HD_B), F32),
                         col < r_t, False)

    def live(carry):
        return jnp.max(carry) > SB_DEAD_LOG

    def cond(st):
        return jnp.logical_and(st[0] < n_chunks, st[1])

    def body(st):
        i, _, carry, acc = st
        slot = slot_of(b, i)

        @pl.when(i + 1 < n_chunks)
        def _():
            for cp in copies(b, i + 1):
                cp.start()

        for cp in copies(b, i):
            cp.wait()
        for pg in reversed(range(SB_PAGES)):
            carry, acc = block(kbuf[slot, pg].astype(BF16), vbuf[slot, pg].astype(BF16),
                               carry, acc, None, True)
        return i + 1, live(carry), carry, acc

    n_done, _, _, acc = lax.while_loop(cond, body, (jnp.int32(0), live(carry0), carry0, acc0))

    @pl.when(n_done < n_chunks)
    def _():
        for cp in copies(b, n_done):
            cp.wait()

    acc = jnp.where(own, acc, 0.0)
    o = acc[:t_new]
    for h in range(1, H_B):
        o = o + acc[h * t_new:(h + 1) * t_new]
    o_ref[0] = o


def _sb_sample(page_table, qb_f32, kn_pad, vn_pad, tri, cache_k, cache_v, layer):
    nb, t_new, _ = qb_f32.shape
    n_pages = page_table.shape[1]
    page = cache_k.shape[3]
    assert tri.shape == (page, page) and kn_pad.shape[1] == page
    grid_spec = pltpu.PrefetchScalarGridSpec(
        num_scalar_prefetch=1,
        grid=(nb,),
        in_specs=[
            pl.BlockSpec((1, t_new, HD_B), lambda b, pt: (b, 0, 0)),
            pl.BlockSpec((1,) + kn_pad.shape[1:], lambda b, pt: (b, 0, 0)),
            pl.BlockSpec((1,) + vn_pad.shape[1:], lambda b, pt: (b, 0, 0)),
            pl.BlockSpec(tri.shape, lambda b, pt: (0, 0)),
            pl.BlockSpec(memory_space=pl.ANY),
            pl.BlockSpec(memory_space=pl.ANY),
        ],
        out_specs=pl.BlockSpec((1, t_new, HD_B), lambda b, pt: (b, 0, 0)),
        scratch_shapes=[
            pltpu.VMEM((4, SB_PAGES, HD_B, page), F32),
            pltpu.VMEM((4, SB_PAGES, HD_B, page), F32),
            pltpu.SemaphoreType.DMA((2, 4)),
        ],
    )
    return pl.pallas_call(
        functools.partial(_sb_sample_kernel, layer=layer, n_pages=n_pages, t_new=t_new),
        grid_spec=grid_spec,
        out_shape=jax.ShapeDtypeStruct((nb, t_new, HD_B), F32),
        compiler_params=_cparams("arbitrary"),
        name="sb_sample",
    )(page_table, qb_f32, kn_pad, vn_pad, tri, cache_k, cache_v)


def _merge_kernel(h_ref, olat_ref, ob_ref, ln_ref, wg_ref, wuv_ref, wba_ref, wbb_ref, wout_ref, o_ref):
    x = h_ref[...]
    d = x.shape[1]
    xn = _rms(x, ln_ref[...]).astype(BF16)
    g = _dot(xn, wg_ref[...])
    o_a = _dot(olat_ref[...], wuv_ref[...]).astype(BF16)
    mix = (jax.nn.sigmoid(g[:, :d]) * _dot(o_a, wba_ref[...])
           + jax.nn.sigmoid(g[:, d:]) * _dot(ob_ref[...], wbb_ref[...]))
    o_ref[...] = x + _dot(mix.astype(BF16), wout_ref[...])


def _merge(h, o_lat, o_b, ln, w_g, w_uvbd, w_br_a, w_br_b, w_out):
    n, d = h.shape
    tm = min(ROW_TILE, n)
    row = lambda i: (i, 0)
    const = lambda i: (0, 0)
    return pl.pallas_call(
        _merge_kernel,
        grid=(n // tm,),
        in_specs=[
            pl.BlockSpec((tm, d), row),
            pl.BlockSpec((tm, o_lat.shape[1]), row),
            pl.BlockSpec((tm, o_b.shape[1]), row),
            pl.BlockSpec((1, d), const),
            pl.BlockSpec(w_g.shape, const),
            pl.BlockSpec(w_uvbd.shape, const),
            pl.BlockSpec(w_br_a.shape, const),
            pl.BlockSpec(w_br_b.shape, const),
            pl.BlockSpec(w_out.shape, const),
        ],
        out_specs=pl.BlockSpec((tm, d), row),
        out_shape=jax.ShapeDtypeStruct((n, d), F32),
        compiler_params=_cparams("arbitrary"),
        name="merge",
    )(h, o_lat, o_b, ln, w_g, w_uvbd, w_br_a, w_br_b, w_out)


def _ffn_kernel(*refs, d_ff, seq_tiles, seq_len, has_state, final):
    if has_state:
        (h_ref, p_ref, e1_ref, e2_ref, lnf_ref, wup_ref, cw_ref, cb_ref, wdn_ref, lnp_ref, wpg_ref,
         wpp_ref, lnz_ref, o_ref, tail_ref, act_ref, carry_ref) = refs
    else:
        (h_ref, p_ref, lnf_ref, wup_ref, cw_ref, cb_ref, wdn_ref, lnp_ref, wpg_ref,
         wpp_ref, lnz_ref, o_ref, tail_ref, act_ref, carry_ref) = refs
    i = pl.program_id(0)
    x = h_ref[...]
    tm = x.shape[0]
    xn = _rms(x, lnf_ref[...]).astype(BF16)
    row = lax.broadcasted_iota(jnp.int32, (tm, FF_CHUNK), 0) % seq_len
    seq_start = i % seq_tiles == 0

    @pl.when(i == 0)
    def _():
        carry_ref[...] = jnp.zeros(carry_ref.shape, F32)

    def conv(u, cols):
        if has_state:
            e1, e2 = e1_ref[:, cols], e2_ref[:, cols]
        else:
            c0 = jnp.where(seq_start, 0.0, carry_ref[SUBLANES - 2:SUBLANES - 1, cols])
            c1 = jnp.where(seq_start, 0.0, carry_ref[SUBLANES - 1:SUBLANES, cols])
            e1 = jnp.broadcast_to(c1, u.shape)
            e2 = jnp.where(row == 0, c0, c1)
        u1 = jnp.where(row < 1, e1, pltpu.roll(u, 1, 0))
        u2 = jnp.where(row < 2, e2, pltpu.roll(u, 2, 0))
        y = cb_ref[:, cols] + u2 * cw_ref[0:1, cols]
        y = y + u1 * cw_ref[1:2, cols]
        y = y + u * cw_ref[2:3, cols]
        if has_state:
            tail_ref[:, cols] = u
        else:
            last8 = u[tm - SUBLANES:, :]
            tail_ref[0, :, cols] = last8
            carry_ref[:, cols] = last8
        return y

    for c in range(d_ff // FF_CHUNK):
        gc = slice(c * FF_CHUNK, (c + 1) * FF_CHUNK)
        vc = slice(d_ff + c * FF_CHUNK, d_ff + (c + 1) * FF_CHUNK)
        gate = conv(_dot(xn, wup_ref[:, gc]), gc)
        val = conv(_dot(xn, wup_ref[:, vc]), vc)
        act_ref[:, gc] = (gate * jax.nn.sigmoid(gate) * val).astype(BF16)

    y = x + _dot(act_ref[...], wdn_ref[...])
    yn = _rms(y, lnp_ref[...]).astype(BF16)
    y = y + jax.nn.sigmoid(_dot(yn, wpg_ref[...])) * _dot(p_ref[...].astype(BF16), wpp_ref[...])
    if final:
        y = _rms(y, lnz_ref[...])
    o_ref[...] = y


def _ffn(h, p, state_e, ln_ffn, w_up, conv_w, conv_b, w_down, ln_ple, w_pg, w_pp, ln_final, *,
         n_seq, final):
    n, d = h.shape
    c2 = w_up.shape[1]
    d_ff = c2 // 2
    tm = min(ROW_TILE, n)
    seq_len = n // n_seq
    has_state = state_e is not None
    row = lambda i: (i, 0)
    if has_state:
        seq_tiles = 1
        tail_spec = pl.BlockSpec((tm, c2), row)
        tail_shape = jax.ShapeDtypeStruct((n, c2), F32)
    else:
        seq_tiles = seq_len // tm
        tail_spec = pl.BlockSpec((1, SUBLANES, c2), lambda i: (i // seq_tiles, 0, 0))
        tail_shape = jax.ShapeDtypeStruct((n_seq, SUBLANES, c2), F32)
    const = lambda i: (0, 0)
    in_specs = [pl.BlockSpec((tm, d), row), pl.BlockSpec((tm, p.shape[1]), row)]
    args = [h, p]
    if has_state:
        in_specs += [pl.BlockSpec((tm, c2), row), pl.BlockSpec((tm, c2), row)]
        args += list(state_e)
    in_specs += [
        pl.BlockSpec((1, d), const),
        pl.BlockSpec(w_up.shape, const),
        pl.BlockSpec(conv_w.shape, const),
        pl.BlockSpec((1, c2), const),
        pl.BlockSpec(w_down.shape, const),
        pl.BlockSpec((1, d), const),
        pl.BlockSpec(w_pg.shape, const),
        pl.BlockSpec(w_pp.shape, const),
        pl.BlockSpec((1, d), const),
    ]
    args += [ln_ffn, w_up, conv_w, conv_b, w_down, ln_ple, w_pg, w_pp, ln_final]
    return pl.pallas_call(
        functools.partial(_ffn_kernel, d_ff=d_ff, seq_tiles=seq_tiles,
                          seq_len=seq_len if has_state else tm * seq_tiles,
                          has_state=has_state, final=final),
        grid=(n // tm,),
        in_specs=in_specs,
        out_specs=[pl.BlockSpec((tm, d), row), tail_spec],
        out_shape=[jax.ShapeDtypeStruct((n, d), F32), tail_shape],
        scratch_shapes=[pltpu.VMEM((tm, d_ff), BF16), pltpu.VMEM((SUBLANES, c2), F32)],
        compiler_params=_cparams("arbitrary"),
        name="ffn",
    )(*args)


def _rope_tables(pos):
    half = ROPE // 2
    inv = ROPE_THETA ** (-jnp.arange(half, dtype=F32) / half)
    ang = pos.astype(F32)[:, None] * inv[None, :]
    cos, sin = jnp.cos(ang), jnp.sin(ang)
    z = jnp.zeros((pos.shape[0], LANES - ROPE), F32)
    zh = jnp.zeros_like(sin)
    return (jnp.concatenate([cos, cos, z], axis=1),
            jnp.concatenate([-sin, zh, z], axis=1),
            jnp.concatenate([zh, sin, z], axis=1))


def _prep_weights(w_in, w_uq, w_uk, w_uv):
    depth, d, _ = w_in.shape
    o_kr = Q_LORA + KV_LORA
    o_qb = o_kr + ROPE
    o_g = o_qb + 3 * HD_B
    w_in_p = jnp.concatenate(
        [w_in[:, :, :o_qb], jnp.zeros((depth, d, LANES - ROPE), w_in.dtype), w_in[:, :, o_qb:o_g]],
        axis=2).astype(BF16)
    w_g = w_in[:, :, o_g:].astype(BF16)
    uq = w_uq.reshape(depth, Q_LORA, H_A, NOPE + ROPE)
    uq_nope = uq[..., :NOPE].reshape(depth, Q_LORA, H_A * NOPE)
    uq_rope = jnp.pad(uq[..., NOPE:], ((0, 0), (0, 0), (0, 0), (0, LANES - ROPE)))
    w_uq_p = jnp.concatenate([uq_nope, uq_rope.reshape(depth, Q_LORA, H_A * LANES)], axis=2).astype(BF16)
    eye = jnp.eye(H_A, dtype=w_uk.dtype)
    w_ukbd = jnp.einsum('lrhn,hg->lhngr', w_uk, eye).reshape(depth, H_A * NOPE, H_A * KV_LORA).astype(BF16)
    w_uvbd = jnp.einsum('lrhv,hg->lhrgv', w_uv, eye).reshape(depth, H_A * KV_LORA, H_A * V_DIM).astype(BF16)
    return w_in_p, w_g, w_uq_p, w_ukbd, w_uvbd


def kernel(x_prompt, x_sample, cache_mla_latent, cache_mla_rope, cache_sb_k, cache_sb_v, state_ffn_conv,
           page_table, p_prompt, p_sample, ln_mix, w_in, q_norm, w_uq, kv_norm, w_uk, w_uv, w_br_a,
           w_br_b, w_out, ln_ffn, w_up, conv_w, conv_b, w_down, ln_ple, w_ple_gate, w_ple_proj,
           ln_final):
    nb, t, d = x_prompt.shape
    db, dt, _ = x_sample.shape
    depth = w_in.shape[0]
    n_pool, page = cache_sb_k.shape[1], cache_sb_k.shape[2]
    past_len = page_table.shape[1] * page
    c2 = w_up.shape[2]

    w_in_p, w_g, w_uq_p, w_ukbd, w_uvbd = _prep_weights(w_in, w_uq, w_uk, w_uv)
    w_br_a16, w_br_b16, w_out16 = w_br_a.astype(BF16), w_br_b.astype(BF16), w_out.astype(BF16)
    w_up16, w_down16 = w_up.astype(BF16), w_down.astype(BF16)
    w_pg16, w_pp16 = w_ple_gate.astype(BF16), w_ple_proj.astype(BF16)
    cache_kt = jnp.transpose(cache_sb_k, (0, 1, 3, 4, 2)).reshape(depth, n_pool, HD_B, page)
    cache_vt = jnp.transpose(cache_sb_v, (0, 1, 3, 4, 2)).reshape(depth, n_pool, HD_B, page)
    cache_rope_t = jnp.transpose(cache_mla_rope, (0, 1, 3, 2))

    tabs_p = _rope_tables(jnp.arange(t, dtype=jnp.int32))
    tabs_s = tuple(jnp.tile(a, (db, 1)) for a in _rope_tables(past_len + jnp.arange(dt, dtype=jnp.int32)))
    tri_n = max(min(ATT_TILE, t), page)
    tri = (jnp.arange(tri_n)[:, None] >= jnp.arange(tri_n)[None, :]).astype(BF16)
    lnz = ln_final.reshape(1, d)
    pad_new = lambda a: jnp.pad(a.reshape(db, dt, -1), ((0, 0), (0, LANES - dt), (0, 0)))

    hp = x_prompt.reshape(nb * t, d)
    hs = x_sample.reshape(db * dt, d)
    st_p, st_s = [], []
    stacked = (jnp.zeros((depth, nb * t, KV_LORA), F32), jnp.zeros((depth, nb, HD_B, t), F32),
               jnp.zeros((depth, nb, HD_B, t), F32))
    for i in range(depth):
        vec = lambda a: a[i].reshape(1, -1)
        proj_w = (vec(ln_mix), w_in_p[i], vec(q_norm), w_uq_p[i], vec(kv_norm), w_ukbd[i])
        merge_w = (vec(ln_mix), w_g[i], w_uvbd[i], w_br_a16[i], w_br_b16[i], w_out16[i])
        ffn_w = (vec(ln_ffn), w_up16[i], conv_w[i], vec(conv_b), w_down16[i], vec(ln_ple), w_pg16[i],
                 w_pp16[i], lnz)
        final = i == depth - 1

        qcat, kcat, krope, qb, kb16, vb16, ckv_t, *stacked = _in_proj(hp, tabs_p, *proj_w,
                                                                      stacked=tuple(stacked), layer=i)
        o_lat = _mla_prompt(qcat, kcat, ckv_t, nb, t)
        o_b = _sb_prompt(qb, kb16, vb16, tri[:min(ATT_TILE, t), :min(ATT_TILE, t)], nb, t)
        hp = _merge(hp, o_lat, o_b, *merge_w)
        hp, tail = _ffn(hp, p_prompt[i].reshape(nb * t, -1), None, *ffn_w, n_seq=nb, final=final)
        st_p.append((krope.reshape(nb, t, ROPE), tail[:, SUBLANES - (CONV_W - 1):]))

        qcat, kcat, krope, qb, kb16, vb16, _, ckv, kb, vb = _in_proj(hs, tabs_s, *proj_w)
        o_lat = _mla_sample(page_table, qcat.astype(F32).reshape(db, dt, -1), pad_new(kcat),
                            cache_mla_latent, cache_rope_t, i)
        o_b = _sb_sample(page_table, qb.astype(F32).reshape(db, dt, -1), pad_new(kb16), pad_new(vb16),
                         tri[:page, :page], cache_kt, cache_vt, i)
        hs = _merge(hs, o_lat.reshape(db * dt, -1).astype(BF16), o_b.reshape(db * dt, -1).astype(BF16),
                    *merge_w)
        st = state_ffn_conv[i]
        zeros = jnp.zeros((db, dt - 1, c2), F32)
        e1 = jnp.concatenate([st[:, 1:2], zeros], axis=1).reshape(db * dt, c2)
        e2 = jnp.concatenate([st, zeros[:, 1:]], axis=1).reshape(db * dt, c2)
        hs, tail = _ffn(hs, p_sample[i].reshape(db * dt, -1), (e1, e2), *ffn_w, n_seq=db, final=final)
        st_s.append((ckv.reshape(db, dt, KV_LORA), krope.reshape(db, dt, ROPE),
                     kb.reshape(db, dt, H_B, D_B), vb.reshape(db, dt, H_B, D_B),
                     tail.reshape(db, dt, c2)[:, dt - (CONV_W - 1):]))

    y_prompt = hp.reshape(nb, t, d)
    y_sample = hs.reshape(db, dt, d)
    ckv_all, kbt_all, vbt_all = stacked
    heads_last = lambda a: jnp.transpose(a.reshape(depth, nb, H_B, D_B, t), (0, 1, 4, 2, 3))
    outs_p = [ckv_all.reshape(depth, nb, t, KV_LORA), jnp.stack([s[0] for s in st_p]),
              heads_last(kbt_all), heads_last(vbt_all), jnp.stack([s[1] for s in st_p])]
    outs_s = [jnp.stack([s[k] for s in st_s]) for k in range(5)]
    return (y_prompt, y_sample, *outs_p, *outs_s)
```
